```python
import math
import numpy as np
import jax
import jax.numpy as jnp
from jax import lax

D_MODEL = 1024
BATCH = 8
SEQ = 4096
DEPTH = 4

CHUNK = 64
BRANCH_WIDTH = D_MODEL // 2
N_BRANCH = 3
M_HEADS = 4
M_HEAD_DIM = BRANCH_WIDTH // M_HEADS
M_CONV = 4
S5_GROUP = 16
S5_GROUPS = BRANCH_WIDTH // S5_GROUP
S5_STATE = 64
A_HEADS = 8
A_HEAD_DIM = BRANCH_WIDTH // A_HEADS
Q_BLOCK = 128
N_GROUPS = 4
EXPERTS_PER_GROUP = 8
N_EXPERTS = N_GROUPS * EXPERTS_PER_GROUP
EXPERT_HIDDEN = D_MODEL // 2
TOP_K = 2
MOE_BLOCK = 256
LN_EPS = 1e-5
ALPHA = (2 * DEPTH) ** 0.25
BETA = (8 * DEPTH) ** -0.25
IN_SIZES = (2 * BRANCH_WIDTH, BRANCH_WIDTH, BRANCH_WIDTH, M_HEADS, M_HEADS,
            BRANCH_WIDTH, 3 * BRANCH_WIDTH, A_HEADS, N_BRANCH * D_MODEL)
IN_TOTAL = 8 * BRANCH_WIDTH + 2 * M_HEADS + A_HEADS + N_BRANCH * D_MODEL

kernel_name = 'hybrid_mlstm_s5_fox_hmoe_encoder'


def _in_starts():
    return [int(s) for s in np.cumsum((0,) + IN_SIZES)]


def layer_norm(x, g, b):
    xf = x.astype(jnp.float32)
    mu = jnp.mean(xf, axis=-1, keepdims=True)
    var = jnp.mean(jnp.square(xf - mu), axis=-1, keepdims=True)
    return ((xf - mu) * lax.rsqrt(var + LN_EPS)).astype(x.dtype) * g + b


def head_norm(h, w, n_heads):
    bsz, seq, width = h.shape
    hf = h.reshape(bsz, seq, n_heads, width // n_heads).astype(jnp.float32)
    mu = jnp.mean(hf, axis=-1, keepdims=True)
    var = jnp.mean(jnp.square(hf - mu), axis=-1, keepdims=True)
    out = ((hf - mu) * lax.rsqrt(var + LN_EPS)).reshape(bsz, seq, width)
    return out.astype(h.dtype) * w


def causal_depthwise_conv(u, w, b):
    k = w.shape[0]
    out = lax.conv_general_dilated(
        u, w[:, None, :].astype(u.dtype), window_strides=(1,), padding=[(k - 1, 0)],
        dimension_numbers=('NWC', 'WIO', 'NWC'), feature_group_count=u.shape[-1])
    return out + b


def _to_chunks(t):
    bsz, seq, heads = t.shape[:3]
    t = t.reshape((bsz, seq // CHUNK, CHUNK, heads) + t.shape[3:])
    return jnp.moveaxis(t, (1, 3), (0, 2))


def mlstm_chunkwise(q, k, v, i_pre, f_pre):
    bsz, seq, heads, dh = q.shape
    f32 = jnp.float32
    qc = _to_chunks(q.astype(f32))
    kc = _to_chunks(k.astype(f32) * dh ** -0.5)
    vc = _to_chunks(v.astype(f32))
    ic = _to_chunks(i_pre.astype(f32))
    lfc = _to_chunks(jax.nn.log_sigmoid(f_pre.astype(f32)))
    causal = jnp.tril(jnp.ones((CHUNK, CHUNK), dtype=bool))

    def step(carry, inp):
        c_st, n_st, m_st = carry
        q_, k_, v_, i_, lf = inp
        b = jnp.cumsum(lf, axis=-1)
        d_log = jnp.where(causal, b[..., :, None] - b[..., None, :] + i_[..., None, :], -jnp.inf)
        g_log = b + m_st[..., None]
        m_t = jnp.maximum(jnp.max(d_log, axis=-1), g_log)
        w_intra = jnp.exp(d_log - m_t[..., None])
        w_state = jnp.exp(g_log - m_t)
        s = jnp.einsum('bhtd,bhsd->bhts', q_, k_) * w_intra
        num = (jnp.einsum('bhts,bhsd->bhtd', s, v_)
               + w_state[..., None] * jnp.einsum('bhtk,bhkv->bhtv', q_, c_st))
        den = jnp.sum(s, axis=-1) + w_state * jnp.einsum('bhtk,bhk->bht', q_, n_st)
        h = num / jnp.maximum(jnp.abs(den), jnp.exp(-m_t))[..., None]
        b_last = b[..., -1]
        a_log = b_last[..., None] - b + i_
        m_new = jnp.maximum(b_last + m_st, jnp.max(a_log, axis=-1))
        kw = k_ * jnp.exp(a_log - m_new[..., None])[..., None]
        decay = jnp.exp(b_last + m_st - m_new)
        c_new = decay[..., None, None] * c_st + jnp.einsum('bhsk,bhsv->bhkv', kw, v_)
        n_new = decay[..., None] * n_st + jnp.sum(kw, axis=2)
        return (c_new, n_new, m_new), h

    init = (jnp.zeros((bsz, heads, dh, dh), f32), jnp.zeros((bsz, heads, dh), f32),
            jnp.zeros((bsz, heads), f32))
    _, h = lax.scan(step, init, (qc, kc, vc, ic, lfc))
    h = jnp.moveaxis(h, (0, 2), (1, 3)).reshape(bsz, seq, heads * dh)
    return h.astype(q.dtype)


def s5_grouped(u, lam_re, lam_im, log_dt, b_re, b_im, c_re, c_im, d):
    f32 = jnp.float32
    bsz, seq, _ = u.shape
    ug = u.astype(f32).reshape(bsz, seq, S5_GROUPS, S5_GROUP)
    lr, li = lam_re.astype(f32), lam_im.astype(f32)
    dt = jnp.exp(log_dt.astype(f32))[:, None]
    mag = jnp.exp(lr * dt)
    abar_re, abar_im = mag * jnp.cos(li * dt), mag * jnp.sin(li * dt)
    den = lr * lr + li * li
    nr, ni = abar_re - 1.0, abar_im
    zr = (nr * lr + ni * li) / den
    zi = (ni * lr - nr * li) / den
    br, bi = b_re.astype(f32), b_im.astype(f32)
    bbar_re = zr[..., None] * br - zi[..., None] * bi
    bbar_im = zr[..., None] * bi + zi[..., None] * br
    xr = jnp.einsum('gpc,bsgc->bsgp', bbar_re, ug)
    xi = jnp.einsum('gpc,bsgc->bsgp', bbar_im, ug)
    a_re = jnp.broadcast_to(abar_re, (1, seq) + abar_re.shape)
    a_im = jnp.broadcast_to(abar_im, (1, seq) + abar_im.shape)

    def combine(e1, e2):
        a1r, a1i, b1r, b1i = e1
        a2r, a2i, b2r, b2i = e2
        return (a1r * a2r - a1i * a2i, a1r * a2i + a1i * a2r,
                a2r * b1r - a2i * b1i + b2r, a2r * b1i + a2i * b1r + b2i)

    _, _, hr, hi = lax.associative_scan(combine, (a_re, a_im, xr, xi), axis=1)
    y = (jnp.einsum('gcp,bsgp->bsgc', c_re.astype(f32), hr)
         - jnp.einsum('gcp,bsgp->bsgc', c_im.astype(f32), hi)
         + d.astype(f32) * ug)
    return y.reshape(bsz, seq, S5_GROUPS * S5_GROUP).astype(u.dtype)


def forgetting_attention(q, k, v, f_pre):
    bsz, seq, heads, dh = q.shape
    n_blk = seq // Q_BLOCK
    f_cum = jnp.cumsum(jax.nn.log_sigmoid(f_pre.astype(jnp.float32)), axis=1)
    f_cum = jnp.transpose(f_cum, (0, 2, 1))
    q_blk = jnp.moveaxis(q.reshape(bsz, n_blk, Q_BLOCK, heads, dh), 1, 0)
    f_blk = jnp.moveaxis(f_cum.reshape(bsz, heads, n_blk, Q_BLOCK), 2, 0)
    k_pos = jnp.arange(seq)
    scale = dh ** -0.5

    def one_block(args):
        qb, fb, bi = args
        q_pos = bi * Q_BLOCK + jnp.arange(Q_BLOCK)
        s = jnp.einsum('bqhd,bkhd->bhqk', qb, k).astype(jnp.float32) * scale
        s = s + fb[..., :, None] - f_cum[..., None, :]
        s = jnp.where(k_pos[None, :] <= q_pos[:, None], s, -jnp.inf)
        p = jax.nn.softmax(s, axis=-1).astype(v.dtype)
        return jnp.einsum('bhqk,bkhd->bqhd', p, v)

    out = lax.map(one_block, (q_blk, f_blk, jnp.arange(n_blk)))
    return jnp.moveaxis(out, 0, 1).reshape(bsz, seq, heads * dh)


def hier_moe(x, w_rg, b_rg, w_re, b_re, w_gate, w_up, w_down):
    bsz, seq, d = x.shape
    n_tok = bsz * seq
    xf = x.reshape(n_tok, d)
    lg = (xf @ w_rg).astype(jnp.float32) + b_rg.astype(jnp.float32)
    grp = jnp.argmax(lg, axis=-1)
    gate_g = jnp.take_along_axis(jax.nn.softmax(lg, axis=-1), grp[:, None], axis=-1)[:, 0]
    le = ((xf @ w_re).astype(jnp.float32) + b_re.astype(jnp.float32)).reshape(
        n_tok, N_GROUPS, EXPERTS_PER_GROUP)
    le = jnp.take_along_axis(le, grp[:, None, None], axis=1)[:, 0]
    top_l, top_i = lax.top_k(le, TOP_K)
    weights = gate_g[:, None] * jax.nn.softmax(top_l, axis=-1)
    expert = grp[:, None] * EXPERTS_PER_GROUP + top_i

    n_asg = n_tok * TOP_K
    flat_e = expert.reshape(n_asg)
    flat_t = jnp.repeat(jnp.arange(n_tok, dtype=jnp.int32), TOP_K)
    flat_w = weights.reshape(n_asg)
    order = jnp.argsort(flat_e)
    se = flat_e[order]
    counts = jnp.zeros((N_EXPERTS,), jnp.int32).at[flat_e].add(1)
    starts = jnp.cumsum(counts) - counts
    pcounts = (counts + MOE_BLOCK - 1) // MOE_BLOCK * MOE_BLOCK
    pends = jnp.cumsum(pcounts)
    pstarts = pends - pcounts
    dest = pstarts[se] + jnp.arange(n_asg, dtype=jnp.int32) - starts[se]
    n_blk = -(-n_asg // MOE_BLOCK) + N_EXPERTS
    buf_tok = jnp.full((n_blk * MOE_BLOCK,), n_tok, jnp.int32).at[dest].set(flat_t[order])
    buf_w = jnp.zeros((n_blk * MOE_BLOCK,), jnp.float32).at[dest].set(flat_w[order])
    blk_e = jnp.minimum(jnp.searchsorted(pends, jnp.arange(n_blk) * MOE_BLOCK, side='right'),
                        N_EXPERTS - 1)
    x_pad = jnp.concatenate([xf, jnp.zeros((1, d), xf.dtype)], axis=0)
    xb = x_pad[buf_tok].reshape(n_blk, MOE_BLOCK, d)

    def expert_block(args):
        xe, e = args
        h = jax.nn.silu(xe @ w_gate[e]) * (xe @ w_up[e])
        return h @ w_down[e]

    yb = lax.map(expert_block, (xb, blk_e)).reshape(n_blk * MOE_BLOCK, d)
    y = jax.ops.segment_sum(yb * buf_w[:, None].astype(yb.dtype), buf_tok,
                            num_segments=n_tok + 1)[:n_tok]
    return y.reshape(bsz, seq, d)


def setup_inputs(seed: int = 0) -> dict:
    key = jax.random.key(seed)
    ks = list(jax.random.split(key, 32))
    f32 = jnp.float32
    L, D, W = DEPTH, D_MODEL, BRANCH_WIDTH
    G, P, C = S5_GROUPS, S5_STATE, S5_GROUP
    E, F = N_EXPERTS, EXPERT_HIDDEN

    def nrm(i, shape, std):
        return jax.random.normal(ks[i], shape, f32) * std

    st = _in_starts()
    b_in = nrm(2, (L, IN_TOTAL), 0.02)
    b_in = b_in.at[:, st[4]:st[4] + M_HEADS].add(jnp.linspace(3.0, 6.0, M_HEADS, dtype=f32))
    b_in = b_in.at[:, st[7]:st[7] + A_HEADS].add(jnp.linspace(1.0, 5.0, A_HEADS, dtype=f32))
    return {
        'x': nrm(0, (BATCH, SEQ, D), 1.0),
        'w_in': nrm(1, (L, D, IN_TOTAL), D ** -0.5),
        'b_in': b_in,
        'm_conv_w': nrm(3, (L, M_CONV, 2 * W), M_CONV ** -0.5),
        'm_conv_b': nrm(4, (L, 2 * W), 0.02),
        'm_norm_w': 1.0 + nrm(5, (L, W), 0.02),
        's5_lam_re': -0.5 + nrm(6, (L, G, P), 0.01),
        's5_lam_im': math.pi * jnp.arange(P, dtype=f32) + nrm(7, (L, G, P), 0.01),
        's5_log_dt': jax.random.uniform(ks[8], (L, G), f32, math.log(1e-3), math.log(1e-1)),
        's5_b_re': nrm(9, (L, G, P, C), (2 * C) ** -0.5),
        's5_b_im': nrm(10, (L, G, P, C), (2 * C) ** -0.5),
        's5_c_re': nrm(11, (L, G, C, P), (2 * P) ** -0.5),
        's5_c_im': nrm(12, (L, G, C, P), (2 * P) ** -0.5),
        's5_d': nrm(13, (L, G, C), 1.0),
        's5_w_glu': nrm(14, (L, W, W), W ** -0.5),
        's5_b_glu': nrm(15, (L, W), 0.02),
        'w_branch': nrm(16, (L, N_BRANCH, W, D), W ** -0.5),
        'w_out': nrm(17, (L, D, D), BETA * D ** -0.5),
        'ln1_g': 1.0 + nrm(18, (L, D), 0.02),
        'ln1_b': nrm(19, (L, D), 0.02),
        'w_route_group': nrm(20, (L, D, N_GROUPS), D ** -0.5),
        'b_route_group': nrm(21, (L, N_GROUPS), 0.01),
        'w_route_expert': nrm(22, (L, D, N_EXPERTS), D ** -0.5),
        'b_route_expert': nrm(23, (L, N_EXPERTS), 0.01),
        'moe_w_gate': nrm(24, (L, E, D, F), D ** -0.5),
        'moe_w_up': nrm(25, (L, E, D, F), D ** -0.5),
        'moe_w_down': nrm(26, (L, E, F, D), BETA * F ** -0.5),
        'ln2_g': 1.0 + nrm(27, (L, D), 0.02),
        'ln2_b': nrm(28, (L, D), 0.02),
    }


def reference(x, w_in, b_in, m_conv_w, m_conv_b, m_norm_w, s5_lam_re, s5_lam_im, s5_log_dt,
              s5_b_re, s5_b_im, s5_c_re, s5_c_im, s5_d, s5_w_glu, s5_b_glu, w_branch, w_out,
              ln1_g, ln1_b, w_route_group, b_route_group, w_route_expert, b_route_expert,
              moe_w_gate, moe_w_up, moe_w_down, ln2_g, ln2_b):
    bsz, seq, d = x.shape
    cuts = _in_starts()[1:-1]
    m_shape = (bsz, seq, M_HEADS, M_HEAD_DIM)
    a_shape = (bsz, seq, A_HEADS, A_HEAD_DIM)
    for l in range(DEPTH):
        proj = x @ w_in[l] + b_in[l]
        m_qk, m_v, m_o, m_i, m_f, s_u, a_qkv, a_f, gate_pre = jnp.split(proj, cuts, axis=-1)

        m_qk = jax.nn.silu(causal_depthwise_conv(m_qk, m_conv_w[l], m_conv_b[l]))
        m_q, m_k = jnp.split(m_qk, 2, axis=-1)
        h_m = mlstm_chunkwise(m_q.reshape(m_shape), m_k.reshape(m_shape), m_v.reshape(m_shape),
                              m_i, m_f)
        y_m = jax.nn.sigmoid(m_o) * head_norm(h_m, m_norm_w[l], M_HEADS)

        y_s = jax.nn.gelu(s5_grouped(s_u, s5_lam_re[l], s5_lam_im[l], s5_log_dt[l], s5_b_re[l],
                                     s5_b_im[l], s5_c_re[l], s5_c_im[l], s5_d[l]))
        y_s = y_s * jax.nn.sigmoid(y_s @ s5_w_glu[l] + s5_b_glu[l])

        a_q, a_k, a_v = jnp.split(a_qkv, 3, axis=-1)
        y_a = forgetting_attention(a_q.reshape(a_shape), a_k.reshape(a_shape),
                                   a_v.reshape(a_shape), a_f)

        gates = jax.nn.sigmoid(gate_pre).reshape(bsz, seq, N_BRANCH, d)
        mixed = (gates[:, :, 0] * (y_m @ w_branch[l, 0])
                 + gates[:, :, 1] * (y_s @ w_branch[l, 1])
                 + gates[:, :, 2] * (y_a @ w_branch[l, 2]))
        x = layer_norm(ALPHA * x + mixed @ w_out[l], ln1_g[l], ln1_b[l])

        moe_out = hier_moe(x, w_route_group[l], b_route_group[l], w_route_expert[l],
                           b_route_expert[l], moe_w_gate[l], moe_w_up[l], moe_w_down[l])
        x = layer_norm(ALPHA * x + moe_out, ln2_g[l], ln2_b[l])
    return x
```

```python
import functools
import math

import numpy as np
import jax
import jax.numpy as jnp
from jax import lax
from jax.experimental import pallas as pl
from jax.experimental.pallas import tpu as pltpu

F32 = jnp.float32
BF16 = jnp.bfloat16
I32 = jnp.int32
HIGHEST = lax.Precision.HIGHEST

D_MODEL = 1024
WIDTH = 512
M_HEADS = 4
M_HEAD_DIM = WIDTH // M_HEADS
M_CONV = 4
M_CHUNK = 256
S5_GROUP = 16
S5_GROUPS = WIDTH // S5_GROUP
S5_STATE = 64
S5_STEP = 16
A_HEADS = 8
A_HEAD_DIM = WIDTH // A_HEADS
N_GROUPS = 4
EXPERTS_PER_GROUP = 8
N_EXPERTS = N_GROUPS * EXPERTS_PER_GROUP
EXPERT_HIDDEN = D_MODEL // 2
TOP_K = 2
MOE_BLOCK = 256
LN_EPS = 1e-5
N_GATE_ROWS = 16
LANES = 128
SUBLANES = 8
VMEM_LIMIT = 56 * 1024 * 1024


def _cparams(sem):
    return pltpu.CompilerParams(dimension_semantics=sem, vmem_limit_bytes=VMEM_LIMIT)


def _full(shape):
    nd = len(shape)
    return pl.BlockSpec(shape, lambda *_: (0,) * nd)


def _dot(a, b):
    return jnp.dot(a, b, preferred_element_type=F32)


def _dot_nt(a, b):
    return lax.dot_general(a, b, (((1,), (1,)), ((), ())), preferred_element_type=F32)


def _layer_norm(z, g, b):
    mu = jnp.mean(z, axis=-1, keepdims=True)
    zc = z - mu
    var = jnp.mean(zc * zc, axis=-1, keepdims=True)
    return zc * lax.rsqrt(var + LN_EPS) * g + b


def _inproj_body(x_ref, w_ref, b_ref, wgt_ref, bgt_ref, cw_ref, cb_ref,
                 mq_ref, mk_ref, mv_ref, mo_ref, su_ref, aq_ref, ak_ref, av_ref, gt_ref,
                 carry_ref, scr_ref, *, tiles_per_seq, tm):
    i = pl.program_id(0)
    xb = x_ref[...].astype(BF16)

    def proj(c0, c1):
        return _dot(xb, w_ref[:, c0:c1]) + b_ref[:, c0:c1]

    u = proj(0, 2 * WIDTH)

    @pl.when(i % tiles_per_seq == 0)
    def _():
        carry_ref[...] = jnp.zeros_like(carry_ref)

    scr_ref[0:SUBLANES, :] = carry_ref[...]
    scr_ref[SUBLANES:, :] = u
    carry_ref[...] = scr_ref[tm:tm + SUBLANES, :]
    acc = u * cw_ref[M_CONV - 1:M_CONV, :] + cb_ref[...]
    for j in range(1, M_CONV):
        acc = acc + scr_ref[SUBLANES - j:SUBLANES - j + tm, :] * cw_ref[M_CONV - 1 - j:M_CONV - j, :]
    a = acc * jax.nn.sigmoid(acc)
    mq_ref[...] = a[:, :WIDTH].astype(BF16)
    mk_ref[...] = (a[:, WIDTH:] * (M_HEAD_DIM ** -0.5)).astype(BF16)
    mv_ref[...] = proj(2 * WIDTH, 3 * WIDTH).astype(BF16)
    mo_ref[...] = proj(3 * WIDTH, 4 * WIDTH)
    su_ref[...] = proj(4 * WIDTH, 5 * WIDTH).astype(BF16)
    aq_ref[...] = (proj(5 * WIDTH, 6 * WIDTH) * (A_HEAD_DIM ** -0.5)).astype(BF16)
    ak_ref[...] = proj(6 * WIDTH, 7 * WIDTH).astype(BF16)
    av_ref[...] = proj(7 * WIDTH, 8 * WIDTH).astype(BF16)
    gt_ref[...] = _dot_nt(wgt_ref[...], xb) + bgt_ref[...]


def _inproj(x2, w_main, b_main, wgt, bgt, conv_w, conv_b, seq):
    n_tok = x2.shape[0]
    tm = 512
    n_main = w_main.shape[1]
    row = lambda i: (i, 0)
    bf = lambda: jax.ShapeDtypeStruct((n_tok, WIDTH), BF16)
    out_shape = (bf(), bf(), bf(), jax.ShapeDtypeStruct((n_tok, WIDTH), F32), bf(), bf(), bf(), bf(),
                 jax.ShapeDtypeStruct((N_GATE_ROWS, n_tok), F32))
    wspec = pl.BlockSpec((tm, WIDTH), row)
    return pl.pallas_call(
        functools.partial(_inproj_body, tiles_per_seq=seq // tm, tm=tm),
        out_shape=out_shape,
        grid=(n_tok // tm,),
        in_specs=[pl.BlockSpec((tm, D_MODEL), row), _full((D_MODEL, n_main)), _full((1, n_main)),
                  _full((N_GATE_ROWS, D_MODEL)), _full((N_GATE_ROWS, 1)),
                  _full((M_CONV, 2 * WIDTH)), _full((1, 2 * WIDTH))],
        out_specs=(wspec,) * 8 + (pl.BlockSpec((N_GATE_ROWS, tm), lambda i: (0, i)),),
        scratch_shapes=[pltpu.VMEM((SUBLANES, 2 * WIDTH), F32), pltpu.VMEM((tm + SUBLANES, 2 * WIDTH), F32)],
        compiler_params=_cparams(("arbitrary",)),
        name="inproj",
    )(x2, w_main, b_main, wgt, bgt, conv_w, conv_b)


def _gateprep_body(g_ref, tri_ref, o_ref, *, seq):
    rowid = lax.broadcasted_iota(I32, (N_GATE_ROWS, M_CHUNK), 0)
    carry = jnp.zeros((N_GATE_ROWS, 1), F32)
    for j in range(seq // M_CHUNK):
        g = g_ref[:, j * M_CHUNK:(j + 1) * M_CHUNK]
        ls = jnp.minimum(g, 0.0) - jnp.log1p(jnp.exp(-jnp.abs(g)))
        c = jnp.dot(ls, tri_ref[...], precision=HIGHEST, preferred_element_type=F32)
        run = c + carry
        o_ref[:, j * M_CHUNK:(j + 1) * M_CHUNK] = jnp.where(rowid < M_HEADS, g,
                                                            jnp.where(rowid < 2 * M_HEADS, c, run))
        carry = run[:, M_CHUNK - 1:M_CHUNK]


def _gateprep(gt, tri_incl, batch, seq):
    return pl.pallas_call(
        functools.partial(_gateprep_body, seq=seq),
        out_shape=jax.ShapeDtypeStruct(gt.shape, F32),
        grid=(batch,),
        in_specs=[pl.BlockSpec((N_GATE_ROWS, seq), lambda b: (0, b)), _full((M_CHUNK, M_CHUNK))],
        out_specs=pl.BlockSpec((N_GATE_ROWS, seq), lambda b: (0, b)),
        compiler_params=_cparams(("parallel",)),
        name="gateprep",
    )(gt, tri_incl)


def _mlstm_body(q_ref, k_ref, v_ref, o_ref, gr_ref, gc_ref, nw_ref, y_ref, c_ref, n_ref, m_ref):
    L = M_CHUNK

    @pl.when(pl.program_id(1) == 0)
    def _():
        c_ref[...] = jnp.zeros_like(c_ref)
        n_ref[...] = jnp.zeros_like(n_ref)
        m_ref[...] = jnp.zeros_like(m_ref)

    causal = lax.broadcasted_iota(I32, (L, L), 0) >= lax.broadcasted_iota(I32, (L, L), 1)
    for h in range(M_HEADS):
        sl = slice(h * M_HEAD_DIM, (h + 1) * M_HEAD_DIM)
        q = q_ref[:, sl]
        k = k_ref[:, sl]
        v = v_ref[:, sl]
        i_row = gr_ref[h:h + 1, :]
        b_row = gr_ref[M_HEADS + h:M_HEADS + h + 1, :]
        i_col = gc_ref[:, h:h + 1]
        b_col = gc_ref[:, M_HEADS + h:M_HEADS + h + 1]
        m_st = m_ref[h:h + 1, 0:1]
        b_last = b_col[L - 1:L, :]

        d_log = jnp.where(causal, b_col + (i_row - b_row), -jnp.inf)
        g_log = b_col + m_st
        m_t = jnp.maximum(jnp.max(d_log, axis=-1, keepdims=True), g_log)
        w_intra = jnp.exp(d_log - m_t)
        w_state = jnp.exp(g_log - m_t)
        s = _dot_nt(q, k) * w_intra
        c_st = c_ref[h]
        num = _dot(s.astype(BF16), v) + w_state * _dot(q, c_st.astype(BF16))
        qn = jnp.sum(q.astype(F32) * n_ref[h:h + 1, :], axis=-1, keepdims=True)
        den = jnp.sum(s, axis=-1, keepdims=True) + w_state * qn
        hh = num / jnp.maximum(jnp.abs(den), jnp.exp(-m_t))

        a_log = b_last - b_col + i_col
        m_new = jnp.maximum(b_last + m_st, jnp.max(a_log, axis=0, keepdims=True))
        kw = k.astype(F32) * jnp.exp(a_log - m_new)
        decay = jnp.exp(b_last + m_st - m_new)
        c_ref[h] = decay * c_st + _dot(kw.T.astype(BF16), v)
        n_ref[h:h + 1, :] = decay * n_ref[h:h + 1, :] + jnp.sum(kw, axis=0, keepdims=True)
        m_ref[h:h + 1, :] = jnp.broadcast_to(m_new, (1, LANES))

        mu = jnp.mean(hh, axis=-1, keepdims=True)
        hc = hh - mu
        var = jnp.mean(hc * hc, axis=-1, keepdims=True)
        hn = hc * lax.rsqrt(var + LN_EPS) * nw_ref[:, sl]
        y_ref[:, sl] = (jax.nn.sigmoid(o_ref[:, sl]) * hn).astype(BF16)


def _mlstm(mq, mk, mv, mo, g_rows, g_cols, norm_w, batch, seq):
    n_tok = mq.shape[0]
    nc = seq // M_CHUNK
    row = lambda b, c: (b * nc + c, 0)
    wspec = pl.BlockSpec((M_CHUNK, WIDTH), row)
    return pl.pallas_call(
        _mlstm_body,
        out_shape=jax.ShapeDtypeStruct((n_tok, WIDTH), BF16),
        grid=(batch, nc),
        in_specs=[wspec, wspec, wspec, wspec,
                  pl.BlockSpec((N_GATE_ROWS, M_CHUNK), lambda b, c: (0, b * nc + c)),
                  pl.BlockSpec((M_CHUNK, N_GATE_ROWS), row),
                  _full((1, WIDTH))],
        out_specs=wspec,
        scratch_shapes=[pltpu.VMEM((M_HEADS, M_HEAD_DIM, M_HEAD_DIM), F32),
                        pltpu.VMEM((SUBLANES, M_HEAD_DIM), F32),
                        pltpu.VMEM((SUBLANES, LANES), F32)],
        compiler_params=_cparams(("parallel", "arbitrary")),
        name="mlstm",
    )(mq, mk, mv, mo, g_rows, g_cols, norm_w)


S5_ROWS = 128
S5_ZW = 2 * 2 * S5_STATE
S5_SCAN_COLS = 1024


def _s5_body(u_ref, mt_ref, et_ref, ft_ref, a1_ref, a2_ref, y_ref, z_ref, hp_ref, h_ref, *, batch):
    @pl.when(pl.program_id(0) == 0)
    def _():
        h_ref[...] = jnp.zeros_like(h_ref)

    for g in range(S5_GROUPS):
        z_ref[:, g * S5_ZW:(g + 1) * S5_ZW] = _dot(u_ref[g], et_ref[g])

    def swap_halves(xv):
        parts = []
        for j in range(S5_SCAN_COLS // S5_ZW):
            parts.append(xv[:, j * S5_ZW + LANES:(j + 1) * S5_ZW])
            parts.append(xv[:, j * S5_ZW:j * S5_ZW + LANES])
        return jnp.concatenate(parts, axis=1)

    for blk in range(S5_GROUPS * S5_ZW // S5_SCAN_COLS):
        cols = slice(blk * S5_SCAN_COLS, (blk + 1) * S5_SCAN_COLS)
        a1 = jnp.broadcast_to(a1_ref[:, cols], (batch, S5_SCAN_COLS))
        a2 = jnp.broadcast_to(a2_ref[:, cols], (batch, S5_SCAN_COLS))

        def step(c, hv, cols=cols, a1=a1, a2=a2):
            r = pl.multiple_of(c * batch, batch)
            hp_ref[pl.ds(r, batch), cols] = hv
            return a1 * hv + a2 * swap_halves(hv) + z_ref[pl.ds(r, batch), cols]

        h_ref[:, cols] = lax.fori_loop(0, S5_ROWS // batch, step, h_ref[:, cols])

    for g in range(S5_GROUPS):
        hp = hp_ref[:, g * S5_ZW:g * S5_ZW + LANES].astype(BF16)
        y_ref[g] = _dot(u_ref[g], mt_ref[g]) + _dot(hp, ft_ref[g])


def _s5(ur, mt, et, ft, a1, a2, batch):
    n_rows = ur.shape[1]
    blk = lambda i: (0, i, 0)
    zcols = S5_GROUPS * S5_ZW
    return pl.pallas_call(
        functools.partial(_s5_body, batch=batch),
        out_shape=jax.ShapeDtypeStruct((S5_GROUPS, n_rows, 256), F32),
        grid=(n_rows // S5_ROWS,),
        in_specs=[pl.BlockSpec((S5_GROUPS, S5_ROWS, 256), blk),
                  _full(mt.shape), _full(et.shape), _full(ft.shape), _full(a1.shape), _full(a2.shape)],
        out_specs=pl.BlockSpec((S5_GROUPS, S5_ROWS, 256), blk),
        scratch_shapes=[pltpu.VMEM((S5_ROWS, zcols), F32), pltpu.VMEM((S5_ROWS, zcols), F32),
                        pltpu.VMEM((batch, zcols), F32)],
        compiler_params=_cparams(("arbitrary",)),
        name="s5",
    )(ur, mt, et, ft, a1, a2)


def _s5_operators(lam_re, lam_im, log_dt, b_re, b_im, c_re, c_im, d):
    hp = dict(precision=HIGHEST)
    dt = jnp.exp(log_dt)[:, None]
    mag = jnp.exp(lam_re * dt)
    ar, ai = mag * jnp.cos(lam_im * dt), mag * jnp.sin(lam_im * dt)
    den = lam_re * lam_re + lam_im * lam_im
    nr, ni = ar - 1.0, ai
    zr = (nr * lam_re + ni * lam_im) / den
    zi = (ni * lam_re - nr * lam_im) / den
    bbr = zr[..., None] * b_re - zi[..., None] * b_im
    bbi = zr[..., None] * b_im + zi[..., None] * b_re

    def cmul(xr, xi, yr, yi):
        return xr * yr - xi * yi, xr * yi + xi * yr

    pr, pi = [jnp.ones_like(ar)], [jnp.zeros_like(ai)]
    for _ in range(S5_STEP):
        nxt = cmul(pr[-1], pi[-1], ar, ai)
        pr.append(nxt[0])
        pi.append(nxt[1])
    pr, pi = jnp.stack(pr), jnp.stack(pi)

    car = c_re[None] * pr[:S5_STEP, :, None, :] - c_im[None] * pi[:S5_STEP, :, None, :]
    cai = c_re[None] * pi[:S5_STEP, :, None, :] + c_im[None] * pr[:S5_STEP, :, None, :]
    kk = jnp.einsum('kgop,gpc->kgoc', car, bbr, **hp) - jnp.einsum('kgop,gpc->kgoc', cai, bbi, **hp)
    kk = kk.at[0].add(d[:, :, None] * jnp.eye(S5_GROUP, dtype=F32))
    lag = jnp.arange(S5_STEP)[:, None] - jnp.arange(S5_STEP)[None, :]
    m_full = jnp.where((lag >= 0)[:, :, None, None, None], kk[jnp.maximum(lag, 0)], 0.0)
    mt = jnp.transpose(m_full, (2, 1, 4, 0, 3)).reshape(S5_GROUPS, 256, 256)

    wr, wi = pr[S5_STEP - 1::-1][:S5_STEP], pi[S5_STEP - 1::-1][:S5_STEP]
    er = wr[..., None] * bbr[None] - wi[..., None] * bbi[None]
    ei = wr[..., None] * bbi[None] + wi[..., None] * bbr[None]
    er = jnp.transpose(er, (1, 0, 3, 2)).reshape(S5_GROUPS, 256, S5_STATE)
    ei = jnp.transpose(ei, (1, 0, 3, 2)).reshape(S5_GROUPS, 256, S5_STATE)
    et = jnp.concatenate([er, ei, ei, er], axis=-1)

    c1r = c_re[None] * pr[1:, :, None, :] - c_im[None] * pi[1:, :, None, :]
    c1i = c_re[None] * pi[1:, :, None, :] + c_im[None] * pr[1:, :, None, :]
    fr = jnp.transpose(c1r, (1, 3, 0, 2)).reshape(S5_GROUPS, S5_STATE, 256)
    fi = jnp.transpose(-c1i, (1, 3, 0, 2)).reshape(S5_GROUPS, S5_STATE, 256)
    ft = jnp.concatenate([fr, fi], axis=1)

    a16r, a16i = pr[S5_STEP], pi[S5_STEP]
    a1 = jnp.concatenate([a16r] * 4, axis=-1).reshape(1, -1)
    a2 = jnp.concatenate([-a16i, a16i, a16i, -a16i], axis=-1).reshape(1, -1)
    return mt.astype(BF16), et.astype(BF16), ft.astype(BF16), a1, a2


FOX_BLOCK = 512


def _fox_body(qi_ref, ki_ref, q_ref, k_ref, v_ref, fc_ref, fr_ref, o_ref, m_ref, l_ref, acc_ref):
    p = pl.program_id(2)
    qi = qi_ref[p]
    ki = ki_ref[p]
    tq = FOX_BLOCK

    @pl.when(ki == 0)
    def _():
        m_ref[...] = jnp.full_like(m_ref, -jnp.inf)
        l_ref[...] = jnp.zeros_like(l_ref)
        acc_ref[...] = jnp.zeros_like(acc_ref)

    def accumulate(masked):
        for hh in range(2):
            sl = slice(hh * A_HEAD_DIM, (hh + 1) * A_HEAD_DIM)
            s = _dot_nt(q_ref[:, sl], k_ref[:, sl])
            s = s + (fc_ref[:, hh:hh + 1] - fr_ref[hh:hh + 1, :])
            if masked:
                keep = lax.broadcasted_iota(I32, (tq, tq), 0) >= lax.broadcasted_iota(I32, (tq, tq), 1)
                s = jnp.where(keep, s, -jnp.inf)
            m_prev = m_ref[hh]
            m_new = jnp.maximum(m_prev, jnp.max(s, axis=-1, keepdims=True))
            alpha = jnp.exp(m_prev - m_new)
            pr = jnp.exp(s - m_new)
            l_ref[hh] = alpha * l_ref[hh] + jnp.sum(pr, axis=-1, keepdims=True)
            acc_ref[hh] = alpha * acc_ref[hh] + _dot(pr.astype(BF16), v_ref[:, sl])
            m_ref[hh] = m_new

    @pl.when(ki < qi)
    def _():
        accumulate(False)

    @pl.when(ki == qi)
    def _():
        accumulate(True)
        for hh in range(2):
            sl = slice(hh * A_HEAD_DIM, (hh + 1) * A_HEAD_DIM)
            o_ref[:, sl] = (acc_ref[hh] / l_ref[hh]).astype(BF16)


def _fox(aq, ak, av, f_cols, f_rows, batch, seq):
    n_tok = aq.shape[0]
    nb = seq // FOX_BLOCK
    pairs = [(qi, ki) for qi in range(nb) for ki in range(qi + 1)]
    qi_tab = jnp.asarray([p[0] for p in pairs], I32)
    ki_tab = jnp.asarray([p[1] for p in pairs], I32)
    qmap = lambda b, hp, p, qt, kt: (b * nb + qt[p], hp)
    kmap = lambda b, hp, p, qt, kt: (b * nb + kt[p], hp)
    grid_spec = pltpu.PrefetchScalarGridSpec(
        num_scalar_prefetch=2,
        grid=(batch, A_HEADS // 2, len(pairs)),
        in_specs=[pl.BlockSpec((FOX_BLOCK, LANES), qmap),
                  pl.BlockSpec((FOX_BLOCK, LANES), kmap),
                  pl.BlockSpec((FOX_BLOCK, LANES), kmap),
                  pl.BlockSpec((None, FOX_BLOCK, 2), lambda b, hp, p, qt, kt: (hp, b * nb + qt[p], 0)),
                  pl.BlockSpec((None, 2, FOX_BLOCK), lambda b, hp, p, qt, kt: (hp, 0, b * nb + kt[p]))],
        out_specs=pl.BlockSpec((FOX_BLOCK, LANES), qmap),
        scratch_shapes=[pltpu.VMEM((2, FOX_BLOCK, 1), F32), pltpu.VMEM((2, FOX_BLOCK, 1), F32),
                        pltpu.VMEM((2, FOX_BLOCK, A_HEAD_DIM), F32)],
    )
    return pl.pallas_call(
        _fox_body,
        out_shape=jax.ShapeDtypeStruct((n_tok, WIDTH), BF16),
        grid_spec=grid_spec,
        compiler_params=_cparams(("parallel", "parallel", "arbitrary")),
        name="fox",
    )(qi_tab, ki_tab, aq, ak, av, f_cols, f_rows)


def _gelu_tanh(x):
    return 0.5 * x * (1.0 + jnp.tanh(math.sqrt(2.0 / math.pi) * (x + 0.044715 * (x * x * x))))


def _merge_body(x_ref, ym_ref, ys_ref, ya_ref, wg_ref, bg_ref, wglu_ref, bglu_ref, wb_ref, wo_ref,
                g1_ref, b1_ref, wr_ref, br_ref, x1_ref, e_ref, w_ref, *, alpha):
    x = x_ref[...]
    xb = x.astype(BF16)
    ys = _gelu_tanh(ys_ref[...])
    ys = ys * jax.nn.sigmoid(_dot(ys.astype(BF16), wglu_ref[...]) + bglu_ref[...])
    branches = (ym_ref[...], ys.astype(BF16), ya_ref[...])
    mixed = None
    for i, yb in enumerate(branches):
        cols = slice(i * D_MODEL, (i + 1) * D_MODEL)
        gate = jax.nn.sigmoid(_dot(xb, wg_ref[:, cols]) + bg_ref[:, cols])
        term = gate * _dot(yb, wb_ref[i])
        mixed = term if mixed is None else mixed + term
    z = alpha * x + _dot(mixed.astype(BF16), wo_ref[...])
    x1 = _layer_norm(z, g1_ref[...], b1_ref[...])
    x1_ref[...] = x1

    tm = x1.shape[0]
    lg = jnp.dot(x1, wr_ref[...], precision=HIGHEST, preferred_element_type=F32) + br_ref[...]
    lane = lax.broadcasted_iota(I32, (tm, LANES), 1).astype(F32)
    big = float(LANES)

    def first_max(vals):
        top = jnp.max(vals, axis=-1, keepdims=True)
        idx = jnp.min(jnp.where(vals == top, lane, big), axis=-1, keepdims=True)
        return top, idx

    gl = jnp.where(lane < N_GROUPS, lg, -jnp.inf)
    gmax, grp = first_max(gl)
    gate_g = 1.0 / jnp.sum(jnp.exp(gl - gmax), axis=-1, keepdims=True)
    lo = N_GROUPS + grp * EXPERTS_PER_GROUP
    el = jnp.where((lane >= lo) & (lane < lo + EXPERTS_PER_GROUP), lg, -jnp.inf)
    l1, i1 = first_max(el)
    l2, i2 = first_max(jnp.where(lane == i1, -jnp.inf, el))
    p1 = 1.0 / (1.0 + jnp.exp(l2 - l1))
    w1 = gate_g * p1
    w2 = gate_g * (1.0 - p1)
    e_ref[...] = jnp.where(lane == 0.0, i1 - N_GROUPS, jnp.where(lane == 1.0, i2 - N_GROUPS, 0.0)).astype(I32)
    w_ref[...] = jnp.where(lane == 0.0, w1, jnp.where(lane == 1.0, w2, 0.0))


def _merge(x2, ym, ys, ya, wg, bg, wglu, bglu, wb, wo, g1, b1, wr, br, alpha):
    n_tok = x2.shape[0]
    tm = 256
    row = lambda i: (i, 0)
    return pl.pallas_call(
        functools.partial(_merge_body, alpha=alpha),
        out_shape=(jax.ShapeDtypeStruct((n_tok, D_MODEL), F32),
                   jax.ShapeDtypeStruct((n_tok, LANES), I32),
                   jax.ShapeDtypeStruct((n_tok, LANES), F32)),
        grid=(n_tok // tm,),
        in_specs=[pl.BlockSpec((tm, D_MODEL), row), pl.BlockSpec((tm, WIDTH), row),
                  pl.BlockSpec((tm, WIDTH), row), pl.BlockSpec((tm, WIDTH), row),
                  _full(wg.shape), _full(bg.shape), _full(wglu.shape), _full(bglu.shape),
                  _full(wb.shape), _full(wo.shape), _full(g1.shape), _full(b1.shape),
                  _full(wr.shape), _full(br.shape)],
        out_specs=(pl.BlockSpec((tm, D_MODEL), row), pl.BlockSpec((tm, LANES), row),
                   pl.BlockSpec((tm, LANES), row)),
        compiler_params=_cparams(("parallel",)),
        name="merge",
    )(x2, ym, ys, ya, wg, bg, wglu, bglu, wb, wo, g1, b1, wr, br)


RANK_TILE = 256


def _rank_body(e_ref, tri_ref, rank_ref, cnt_ref, carry_ref):
    @pl.when(pl.program_id(0) == 0)
    def _():
        carry_ref[...] = jnp.zeros_like(carry_ref)

    lane = lax.broadcasted_iota(I32, (RANK_TILE, LANES), 1)
    oh1 = lane == e_ref[:, 0:1]
    oh2 = lane == e_ref[:, 1:2]
    oh = jnp.where(oh1, 1.0, 0.0) + jnp.where(oh2, 1.0, 0.0)
    before = _dot(tri_ref[...], oh.astype(BF16)) + carry_ref[...]
    r1 = jnp.sum(jnp.where(oh1, before, 0.0), axis=-1, keepdims=True)
    r2 = jnp.sum(jnp.where(oh2, before, 0.0), axis=-1, keepdims=True)
    rank_ref[...] = jnp.where(lane == 0, r1, jnp.where(lane == 1, r2, 0.0)).astype(I32)
    carry_ref[...] = carry_ref[...] + jnp.sum(oh, axis=0, keepdims=True)
    cnt_ref[...] = carry_ref[...]


def _rank(e_pad, tri_strict):
    n_tok = e_pad.shape[0]
    row = lambda i: (i, 0)
    return pl.pallas_call(
        _rank_body,
        out_shape=(jax.ShapeDtypeStruct((n_tok, LANES), I32), jax.ShapeDtypeStruct((1, LANES), F32)),
        grid=(n_tok // RANK_TILE,),
        in_specs=[pl.BlockSpec((RANK_TILE, LANES), row), _full((RANK_TILE, RANK_TILE))],
        out_specs=(pl.BlockSpec((RANK_TILE, LANES), row), _full((1, LANES))),
        scratch_shapes=[pltpu.VMEM((1, LANES), F32)],
        compiler_params=_cparams(("arbitrary",)),
        name="rank",
    )(e_pad, tri_strict)


DISPATCH_TILE = 512


def _dispatch_body(dest_ref, x_ref, xs_in_ref, xs_ref, sem):
    del xs_in_ref
    base = pl.program_id(0) * DISPATCH_TILE

    def row_copy(r, d):
        return pltpu.make_async_copy(x_ref.at[pl.ds(r, 1)], xs_ref.at[pl.ds(d, 1)], sem)

    def issue(r, _):
        for k in range(TOP_K):
            row_copy(r, dest_ref[(base + r) * TOP_K + k]).start()
        return 0

    lax.fori_loop(0, DISPATCH_TILE, issue, 0)

    def drain(r, _):
        for k in range(TOP_K):
            row_copy(0, 0).wait()
        return 0

    lax.fori_loop(0, DISPATCH_TILE, drain, 0)


def _dispatch(dest_flat, x1, xs_init):
    n_tok = x1.shape[0]
    grid_spec = pltpu.PrefetchScalarGridSpec(
        num_scalar_prefetch=1,
        grid=(n_tok // DISPATCH_TILE,),
        in_specs=[pl.BlockSpec((DISPATCH_TILE, D_MODEL), lambda i, d: (i, 0)),
                  pl.BlockSpec(memory_space=pl.ANY)],
        out_specs=pl.BlockSpec(memory_space=pl.ANY),
        scratch_shapes=[pltpu.SemaphoreType.DMA(())],
    )
    return pl.pallas_call(
        _dispatch_body,
        out_shape=jax.ShapeDtypeStruct(xs_init.shape, xs_init.dtype),
        grid_spec=grid_spec,
        input_output_aliases={2: 0},
        compiler_params=_cparams(("arbitrary",)),
        name="dispatch",
    )(dest_flat, x1, xs_init)


def _experts_body(be_ref, xs_ref, wg_ref, wu_ref, wd_ref, yb_ref, wg_s, wu_s, wd_s):
    j = pl.program_id(0)
    prev = be_ref[jnp.maximum(j - 1, 0)]

    @pl.when((j == 0) | (be_ref[j] != prev))
    def _():
        wg_s[...] = wg_ref[...].astype(BF16)
        wu_s[...] = wu_ref[...].astype(BF16)
        wd_s[...] = wd_ref[...].astype(BF16)

    xe = xs_ref[...].astype(BF16)
    a = _dot(xe, wg_s[...])
    h = (a * jax.nn.sigmoid(a)) * _dot(xe, wu_s[...])
    yb_ref[...] = _dot(h.astype(BF16), wd_s[...])


def _experts(blk_e, xs, w_gate, w_up, w_down):
    n_rows = xs.shape[0]
    wmap = lambda j, be: (be[j], 0, 0)
    grid_spec = pltpu.PrefetchScalarGridSpec(
        num_scalar_prefetch=1,
        grid=(n_rows // MOE_BLOCK,),
        in_specs=[pl.BlockSpec((MOE_BLOCK, D_MODEL), lambda j, be: (j, 0)),
                  pl.BlockSpec((None, D_MODEL, EXPERT_HIDDEN), wmap),
                  pl.BlockSpec((None, D_MODEL, EXPERT_HIDDEN), wmap),
                  pl.BlockSpec((None, EXPERT_HIDDEN, D_MODEL), wmap)],
        out_specs=pl.BlockSpec((MOE_BLOCK, D_MODEL), lambda j, be: (j, 0)),
        scratch_shapes=[pltpu.VMEM((D_MODEL, EXPERT_HIDDEN), BF16), pltpu.VMEM((D_MODEL, EXPERT_HIDDEN), BF16),
                        pltpu.VMEM((EXPERT_HIDDEN, D_MODEL), BF16)],
    )
    return pl.pallas_call(
        _experts_body,
        out_shape=jax.ShapeDtypeStruct((n_rows, D_MODEL), F32),
        grid_spec=grid_spec,
        compiler_params=_cparams(("arbitrary",)),
        name="experts",
    )(blk_e, xs, w_gate, w_up, w_down)


COMBINE_TILE = 256


def _combine_body(dest_ref, x1_ref, w_ref, yb_ref, g2_ref, b2_ref, o_ref, rows_ref, sem, *, alpha):
    base = pl.program_id(0) * COMBINE_TILE

    def row_copy(r, k, d):
        return pltpu.make_async_copy(yb_ref.at[pl.ds(d, 1)], rows_ref.at[k, pl.ds(r, 1)], sem)

    def issue(r, _):
        for k in range(TOP_K):
            row_copy(r, k, dest_ref[(base + r) * TOP_K + k]).start()
        return 0

    lax.fori_loop(0, COMBINE_TILE, issue, 0)

    def drain(r, _):
        for k in range(TOP_K):
            row_copy(0, 0, 0).wait()
        return 0

    lax.fori_loop(0, COMBINE_TILE, drain, 0)
    moe = w_ref[:, 0:1] * rows_ref[0] + w_ref[:, 1:2] * rows_ref[1]
    o_ref[...] = _layer_norm(alpha * x1_ref[...] + moe, g2_ref[...], b2_ref[...])


def _combine(dest_flat, x1, w_pad, yb, g2, b2, alpha):
    n_tok = x1.shape[0]
    row = lambda i, d: (i, 0)
    grid_spec = pltpu.PrefetchScalarGridSpec(
        num_scalar_prefetch=1,
        grid=(n_tok // COMBINE_TILE,),
        in_specs=[pl.BlockSpec((COMBINE_TILE, D_MODEL), row), pl.BlockSpec((COMBINE_TILE, LANES), row),
                  pl.BlockSpec(memory_space=pl.ANY),
                  pl.BlockSpec((1, D_MODEL), lambda i, d: (0, 0)), pl.BlockSpec((1, D_MODEL), lambda i, d: (0, 0))],
        out_specs=pl.BlockSpec((COMBINE_TILE, D_MODEL), row),
        scratch_shapes=[pltpu.VMEM((TOP_K, COMBINE_TILE, D_MODEL), F32), pltpu.SemaphoreType.DMA(())],
    )
    return pl.pallas_call(
        functools.partial(_combine_body, alpha=alpha),
        out_shape=jax.ShapeDtypeStruct((n_tok, D_MODEL), F32),
        grid_spec=grid_spec,
        compiler_params=_cparams(("arbitrary",)),
        name="combine",
    )(dest_flat, x1, w_pad, yb, g2, b2)


def _in_starts():
    sizes = (2 * WIDTH, WIDTH, WIDTH, M_HEADS, M_HEADS, WIDTH, 3 * WIDTH, A_HEADS, 3 * D_MODEL)
    return [int(s) for s in np.cumsum((0,) + sizes)]


def kernel(x, w_in, b_in, m_conv_w, m_conv_b, m_norm_w, s5_lam_re, s5_lam_im, s5_log_dt, s5_b_re, s5_b_im, s5_c_re, s5_c_im, s5_d, s5_w_glu, s5_b_glu, w_branch, w_out, ln1_g, ln1_b, w_route_group, b_route_group, w_route_expert, b_route_expert, moe_w_gate, moe_w_up, moe_w_down, ln2_g, ln2_b):
    batch, seq, d = x.shape
    depth = w_in.shape[0]
    n_tok = batch * seq
    assert d == D_MODEL and batch == SUBLANES
    assert seq % FOX_BLOCK == 0 and seq % M_CHUNK == 0 and seq % (S5_STEP * S5_ROWS // batch) == 0
    alpha = (2 * depth) ** 0.25
    st = _in_starts()
    n_step = seq // S5_STEP
    n_asg = n_tok * TOP_K
    n_blk = -(-n_asg // MOE_BLOCK) + N_EXPERTS

    tri_incl = jnp.triu(jnp.ones((M_CHUNK, M_CHUNK), F32))
    tri_strict = jnp.tril(jnp.ones((RANK_TILE, RANK_TILE), BF16), k=-1)

    def cols(a, *ranges):
        return jnp.concatenate([a[..., lo:hi] for lo, hi in ranges], axis=-1)

    main_cols = ((st[0], st[3]), (st[5], st[7]))
    gate_cols = ((st[3], st[5]), (st[7], st[8]))

    x2 = x.reshape(n_tok, d)
    for l in range(depth):
        w_l, b_l = w_in[l], b_in[l]
        w_main = cols(w_l, *main_cols).astype(BF16)
        b_main = cols(b_l, *main_cols)[None, :]
        wgt = cols(w_l, *gate_cols).T.astype(BF16)
        bgt = cols(b_l, *gate_cols)[:, None]
        wg = w_l[:, st[8]:st[9]].astype(BF16)
        bg = b_l[st[8]:st[9]][None, :]

        mq, mk, mv, mo, su, aq, ak, av, gt = _inproj(x2, w_main, b_main, wgt, bgt, m_conv_w[l],
                                                     m_conv_b[l][None, :], seq)
        g_rows = _gateprep(gt, tri_incl, batch, seq)
        g_cols = g_rows.T

        ym = _mlstm(mq, mk, mv, mo, g_rows, g_cols, m_norm_w[l][None, :], batch, seq)

        mt, et, ft, a1, a2 = _s5_operators(s5_lam_re[l], s5_lam_im[l], s5_log_dt[l], s5_b_re[l], s5_b_im[l],
                                           s5_c_re[l], s5_c_im[l], s5_d[l])
        ur = su.reshape(batch, n_step, S5_STEP, S5_GROUPS, S5_GROUP).transpose(3, 1, 0, 2, 4)
        ur = ur.reshape(S5_GROUPS, n_step * batch, S5_STEP * S5_GROUP)
        yr = _s5(ur, mt, et, ft, a1, a2, batch)
        ys = yr.reshape(S5_GROUPS, n_step, batch, S5_STEP, S5_GROUP).transpose(2, 1, 3, 0, 4)
        ys = ys.reshape(n_tok, WIDTH)

        f_rows = g_rows[2 * M_HEADS:].reshape(A_HEADS // 2, 2, n_tok)
        f_cols = f_rows.transpose(0, 2, 1)
        ya = _fox(aq, ak, av, f_cols, f_rows, batch, seq)

        wr = jnp.zeros((D_MODEL, LANES), F32)
        wr = wr.at[:, :N_GROUPS].set(w_route_group[l]).at[:, N_GROUPS:N_GROUPS + N_EXPERTS].set(w_route_expert[l])
        br = jnp.zeros((1, LANES), F32)
        br = br.at[0, :N_GROUPS].set(b_route_group[l]).at[0, N_GROUPS:N_GROUPS + N_EXPERTS].set(b_route_expert[l])
        x1, e_pad, w_pad = _merge(x2, ym, ys, ya, wg, bg, s5_w_glu[l].astype(BF16), s5_b_glu[l][None, :],
                                  w_branch[l].astype(BF16), w_out[l].astype(BF16),
                                  ln1_g[l][None, :], ln1_b[l][None, :], wr, br, alpha)

        rank_pad, cnt = _rank(e_pad, tri_strict)
        counts = cnt[0, :N_EXPERTS].astype(I32)
        pcounts = (counts + MOE_BLOCK - 1) // MOE_BLOCK * MOE_BLOCK
        pends = jnp.cumsum(pcounts)
        pstarts = pends - pcounts
        dest = (pstarts[e_pad[:, :TOP_K]] + rank_pad[:, :TOP_K]).reshape(n_asg)
        blk_e = jnp.minimum(jnp.sum(pends[None, :] <= (jnp.arange(n_blk, dtype=I32) * MOE_BLOCK)[:, None], axis=1),
                            N_EXPERTS - 1).astype(I32)

        xs = _dispatch(dest, x1, jnp.zeros((n_blk * MOE_BLOCK, D_MODEL), F32))
        yb = _experts(blk_e, xs, moe_w_gate[l], moe_w_up[l], moe_w_down[l])
        x2 = _combine(dest, x1, w_pad, yb, ln2_g[l][None, :], ln2_b[l][None, :], alpha)
    return x2.reshape(batch, seq, d)
```
